```python
import math
import jax, jax.numpy as jnp
from jax import lax
import numpy as np

D_MODEL = 4096
BATCH = 4
SEQ = 2048
DEPTH = 2
DEC_BATCH = 8
DEC_SEQ = 8
PAST_LEN = 16384
PAGE_SIZE = 128

N_MIXERS = 2
N_ATTN_LAYERS = (DEPTH + 1) // 2
N_REC_LAYERS = DEPTH // 2
N_HEADS = 32
HEAD_DIM = D_MODEL // N_HEADS
MOBA_BLOCK = 256
MOBA_TOP_K = 3
Q_CHUNK = 16
LRU_BLOCK = 256
LRU_WIDTH = -(-4 * D_MODEL // (3 * LRU_BLOCK)) * LRU_BLOCK
LRU_N_BLOCKS = LRU_WIDTH // LRU_BLOCK
CONV_WIDTH = 4
LRU_C = 8.0
D_FF = -(-8 * D_MODEL // (3 * 256)) * 256
RMS_EPS = 1e-6

kernel_name = "moba_rglru_hybrid_decode_step"


def rms_norm(x, g):
    xf = x.astype(jnp.float32)
    y = xf * lax.rsqrt(jnp.mean(xf * xf, axis=-1, keepdims=True) + RMS_EPS)
    return (y * g.astype(jnp.float32)).astype(x.dtype)


def swiglu_ffn(x, w_up, w_down):
    g, u = jnp.split(x @ w_up, 2, axis=-1)
    return (jax.nn.silu(g) * u) @ w_down


def to_blocks(parts):
    B, _, H, Dh = parts[-1].shape
    dt = parts[-1].dtype
    L = sum(p.shape[1] for p in parts)
    n_blk = -(-L // MOBA_BLOCK)
    pad = n_blk * MOBA_BLOCK - L
    full = jnp.concatenate([p.astype(dt) for p in parts] + [jnp.zeros((B, pad, H, Dh), dt)], axis=1)
    return full.reshape(B, n_blk, MOBA_BLOCK, H, Dh)


def moba_chunk(q, q_pos, kb, vb, k_mean):
    B, QC, H, Dh = q.shape
    n_blk = kb.shape[1]
    own = q_pos // MOBA_BLOCK
    gate = jnp.einsum('bqhd,bnhd->bqhn', q.astype(jnp.float32), k_mean)
    past_ok = jnp.arange(n_blk)[None, :] < own[:, None]
    gate = jnp.where(past_ok[None, :, None, :], gate, -jnp.inf)
    k_sel = min(MOBA_TOP_K, n_blk)
    _, top_idx = lax.top_k(gate, k_sel)
    own_idx = jnp.broadcast_to(own[None, :, None, None], (B, QC, H, 1)).astype(top_idx.dtype)
    blk_idx = jnp.concatenate([top_idx, own_idx], axis=-1)
    b_i = jnp.arange(B)[:, None, None, None]
    h_i = jnp.arange(H)[None, None, :, None]
    k_g = kb[b_i, blk_idx, :, h_i]
    v_g = vb[b_i, blk_idx, :, h_i]
    key_pos = blk_idx[..., None] * MOBA_BLOCK + jnp.arange(MOBA_BLOCK)
    is_own = (jnp.arange(k_sel + 1) == k_sel)[:, None]
    valid = jnp.where(is_own,
                      key_pos <= q_pos[None, :, None, None, None],
                      blk_idx[..., None] < own[None, :, None, None, None])
    s = jnp.einsum('bqhd,bqhskd->bqhsk', q, k_g, preferred_element_type=jnp.float32) * (HEAD_DIM ** -0.5)
    s = jnp.where(valid, s, -jnp.inf).reshape(B, QC, H, -1)
    p = jax.nn.softmax(s, axis=-1).reshape(valid.shape).astype(v_g.dtype)
    return jnp.einsum('bqhsk,bqhskd->bqhd', p, v_g)


def split_qkv(xn, w_qkv):
    B, T, _ = xn.shape
    qkv = (xn @ w_qkv).reshape(B, T, 3, N_HEADS, HEAD_DIM)
    return qkv[:, :, 0], qkv[:, :, 1], qkv[:, :, 2]


def moba_prompt(xn, w_qkv, w_o):
    B, T, _ = xn.shape
    q, k, v = split_qkv(xn, w_qkv)
    kb, vb = to_blocks([k]), to_blocks([v])
    k_mean = jnp.mean(kb.astype(jnp.float32), axis=2)
    n_chunk = T // Q_CHUNK
    q_c = q.reshape(B, n_chunk, Q_CHUNK, N_HEADS, HEAD_DIM).transpose(1, 0, 2, 3, 4)
    pos_c = jnp.arange(T, dtype=jnp.int32).reshape(n_chunk, Q_CHUNK)
    o = lax.map(lambda a: moba_chunk(a[0], a[1], kb, vb, k_mean), (q_c, pos_c))
    o = o.transpose(1, 0, 2, 3, 4).reshape(B, T, D_MODEL)
    return o @ w_o, k, v


def moba_sample(xn, cache_k, cache_v, layer_j, page_table, w_qkv, w_o):
    B, T, _ = xn.shape
    q, k, v = split_qkv(xn, w_qkv)
    k_past = cache_k[layer_j, page_table].reshape(B, -1, N_HEADS, HEAD_DIM)
    v_past = cache_v[layer_j, page_table].reshape(B, -1, N_HEADS, HEAD_DIM)
    past_len = k_past.shape[1]
    kb, vb = to_blocks([k_past, k]), to_blocks([v_past, v])
    k_mean = jnp.mean(kb.astype(jnp.float32), axis=2)
    q_pos = past_len + jnp.arange(T, dtype=jnp.int32)
    o = moba_chunk(q, q_pos, kb, vb, k_mean).reshape(B, T, D_MODEL)
    return o @ w_o, k, v


def rglru_block(xn, conv_prev, h_prev, w_in, conv_w, conv_b, w_a, b_a, w_i, b_i, lam, w_out):
    B, T, _ = xn.shape
    gate_branch, u = jnp.split(xn @ w_in, 2, axis=-1)
    u_ext = jnp.concatenate([conv_prev.astype(u.dtype), u], axis=1)
    uc = conv_b
    for j in range(CONV_WIDTH):
        uc = uc + u_ext[:, j:j + T] * conv_w[j]
    new_conv = u_ext[:, -(CONV_WIDTH - 1):]
    ub = uc.reshape(B, T, LRU_N_BLOCKS, LRU_BLOCK)
    r = jax.nn.sigmoid(jnp.einsum('btnc,ncd->btnd', ub, w_a).reshape(B, T, LRU_WIDTH) + b_a)
    i = jax.nn.sigmoid(jnp.einsum('btnc,ncd->btnd', ub, w_i).reshape(B, T, LRU_WIDTH) + b_i)
    log_a = -LRU_C * r.astype(jnp.float32) * jax.nn.softplus(-lam.astype(jnp.float32))
    a = jnp.exp(log_a)
    b = jnp.sqrt(-jnp.expm1(2.0 * log_a)) * (i * uc).astype(jnp.float32)
    b = b.at[:, 0].add(a[:, 0] * h_prev.astype(jnp.float32))

    def combine(left, right):
        a_l, b_l = left
        a_r, b_r = right
        return a_l * a_r, a_r * b_l + b_r

    _, h = lax.associative_scan(combine, (a, b), axis=1)
    y = (h * jax.nn.gelu(gate_branch.astype(jnp.float32))).astype(xn.dtype) @ w_out
    return y, new_conv, h[:, -1].astype(h_prev.dtype)


def setup_inputs(seed: int = 0) -> dict:
    key = jax.random.key(seed)
    ks = iter(jax.random.split(key, 32))
    f32 = jnp.float32

    def nrm(shape, scale):
        return jax.random.normal(next(ks), shape, f32) * scale

    n_pages = PAST_LEN // PAGE_SIZE
    n_used = DEC_BATCH * n_pages
    n_pool = n_used + max(1, n_used // 4)
    page_table = jax.random.permutation(next(ks), n_pool)[:n_used].reshape(DEC_BATCH, n_pages).astype(jnp.int32)
    a_init = jax.random.uniform(next(ks), (N_REC_LAYERS, LRU_WIDTH), f32, 0.9, 0.999)
    return {
        "x_prompt": nrm((BATCH, SEQ, D_MODEL), 1.0),
        "x_sample": nrm((DEC_BATCH, DEC_SEQ, D_MODEL), 1.0),
        "cache_k": nrm((N_ATTN_LAYERS, n_pool, PAGE_SIZE, N_HEADS, HEAD_DIM), 1.0),
        "cache_v": nrm((N_ATTN_LAYERS, n_pool, PAGE_SIZE, N_HEADS, HEAD_DIM), 1.0),
        "page_table": page_table,
        "state_conv": nrm((N_REC_LAYERS, DEC_BATCH, CONV_WIDTH - 1, LRU_WIDTH), 1.0),
        "state_h": nrm((N_REC_LAYERS, DEC_BATCH, LRU_WIDTH), 0.5),
        "norm_mix": 1.0 + nrm((DEPTH, D_MODEL), 0.02),
        "norm_ffn": 1.0 + nrm((DEPTH, D_MODEL), 0.02),
        "norm_final": 1.0 + nrm((D_MODEL,), 0.02),
        "w_qkv": nrm((N_ATTN_LAYERS, D_MODEL, 3 * D_MODEL), D_MODEL ** -0.5),
        "w_o": nrm((N_ATTN_LAYERS, D_MODEL, D_MODEL), D_MODEL ** -0.5),
        "w_rec_in": nrm((N_REC_LAYERS, D_MODEL, 2 * LRU_WIDTH), D_MODEL ** -0.5),
        "rec_conv_w": nrm((N_REC_LAYERS, CONV_WIDTH, LRU_WIDTH), CONV_WIDTH ** -0.5),
        "rec_conv_b": nrm((N_REC_LAYERS, LRU_WIDTH), 0.01),
        "w_rec_a": nrm((N_REC_LAYERS, LRU_N_BLOCKS, LRU_BLOCK, LRU_BLOCK), LRU_BLOCK ** -0.5),
        "b_rec_a": nrm((N_REC_LAYERS, LRU_WIDTH), 0.01),
        "w_rec_i": nrm((N_REC_LAYERS, LRU_N_BLOCKS, LRU_BLOCK, LRU_BLOCK), LRU_BLOCK ** -0.5),
        "b_rec_i": nrm((N_REC_LAYERS, LRU_WIDTH), 0.01),
        "rec_lambda": jnp.log(a_init) - jnp.log1p(-a_init),
        "w_rec_out": nrm((N_REC_LAYERS, LRU_WIDTH, D_MODEL), LRU_WIDTH ** -0.5),
        "w_ffn_up": nrm((DEPTH, D_MODEL, 2 * D_FF), D_MODEL ** -0.5),
        "w_ffn_down": nrm((DEPTH, D_FF, D_MODEL), D_FF ** -0.5),
    }


def reference(x_prompt, x_sample, cache_k, cache_v, page_table, state_conv, state_h,
              norm_mix, norm_ffn, norm_final, w_qkv, w_o, w_rec_in, rec_conv_w, rec_conv_b,
              w_rec_a, b_rec_a, w_rec_i, b_rec_i, rec_lambda, w_rec_out, w_ffn_up, w_ffn_down):
    xp, xs = x_prompt, x_sample
    kp_l, vp_l, ks_l, vs_l = [], [], [], []
    cp_l, hp_l, cs_l, hs_l = [], [], [], []
    for layer in range(DEPTH):
        j = layer // N_MIXERS
        np_ = rms_norm(xp, norm_mix[layer])
        ns_ = rms_norm(xs, norm_mix[layer])
        if layer % N_MIXERS == 0:
            yp, kp, vp = moba_prompt(np_, w_qkv[j], w_o[j])
            ys, k_s, v_s = moba_sample(ns_, cache_k, cache_v, j, page_table, w_qkv[j], w_o[j])
            kp_l.append(kp); vp_l.append(vp); ks_l.append(k_s); vs_l.append(v_s)
        else:
            prm = (w_rec_in[j], rec_conv_w[j], rec_conv_b[j], w_rec_a[j], b_rec_a[j],
                   w_rec_i[j], b_rec_i[j], rec_lambda[j], w_rec_out[j])
            conv0 = jnp.zeros((xp.shape[0], CONV_WIDTH - 1, LRU_WIDTH), state_conv.dtype)
            h0 = jnp.zeros((xp.shape[0], LRU_WIDTH), state_h.dtype)
            yp, cp, hp = rglru_block(np_, conv0, h0, *prm)
            ys, c_s, h_s = rglru_block(ns_, state_conv[j], state_h[j], *prm)
            cp_l.append(cp); hp_l.append(hp); cs_l.append(c_s); hs_l.append(h_s)
        xp = xp + yp
        xs = xs + ys
        xp = xp + swiglu_ffn(rms_norm(xp, norm_ffn[layer]), w_ffn_up[layer], w_ffn_down[layer])
        xs = xs + swiglu_ffn(rms_norm(xs, norm_ffn[layer]), w_ffn_up[layer], w_ffn_down[layer])
    y_prompt = rms_norm(xp, norm_final)
    y_sample = rms_norm(xs, norm_final)
    return (y_prompt, y_sample, jnp.stack(kp_l), jnp.stack(vp_l), jnp.stack(ks_l), jnp.stack(vs_l),
            jnp.stack(cp_l), jnp.stack(hp_l), jnp.stack(cs_l), jnp.stack(hs_l))
```

```python
import functools

import jax
import jax.numpy as jnp
from jax import lax
from jax.experimental import pallas as pl
from jax.experimental.pallas import tpu as pltpu

N_HEADS = 32
HEAD_DIM = 128
MOBA_BLOCK = 256
MOBA_BLOCK_LOG2 = 8
MOBA_TOP_K = 3
LRU_BLOCK = 256
CONV_WIDTH = 4
LRU_C = 8.0
RMS_EPS = 1e-6
MASKED = -1e30
SUBLANES = 8
V7X_VMEM_BYTES = 64 * 1024 * 1024
VMEM_CAP = V7X_VMEM_BYTES - 6 * 1024 * 1024

F32 = jnp.float32
BF16 = jnp.bfloat16
NT_DIMS = (((1,), (1,)), ((), ()))


def _params(block_bytes, temp_bytes=0):
    need = 2 * block_bytes + temp_bytes + (4 << 20)
    return pltpu.CompilerParams(vmem_limit_bytes=int(min(max(need, 16 << 20), VMEM_CAP)))


def _nbytes(shape, dtype):
    n = 1
    for s in shape:
        n *= s
    return n * jnp.dtype(dtype).itemsize


def _rmsnorm_kernel(x_ref, g_ref, o_ref):
    x = x_ref[...]
    y = x * lax.rsqrt(jnp.mean(x * x, axis=-1, keepdims=True) + RMS_EPS)
    o_ref[...] = (y * g_ref[...]).astype(o_ref.dtype)


def _rmsnorm(x, g, out_dtype):
    m, d = x.shape
    tm = min(m, 512)
    blocks = _nbytes((tm, d), F32) + _nbytes((tm, d), out_dtype)
    return pl.pallas_call(
        _rmsnorm_kernel,
        grid=(m // tm,),
        in_specs=[pl.BlockSpec((tm, d), lambda i: (i, 0)),
                  pl.BlockSpec((1, d), lambda i: (0, 0))],
        out_specs=pl.BlockSpec((tm, d), lambda i: (i, 0)),
        out_shape=jax.ShapeDtypeStruct((m, d), out_dtype),
        compiler_params=_params(blocks, 2 * _nbytes((tm, d), F32)),
        name="rmsnorm",
    )(x, g.reshape(1, d))


def _mm_kernel(x_ref, w_ref, o_ref):
    o_ref[...] = jnp.dot(x_ref[...], w_ref[...],
                         preferred_element_type=F32).astype(o_ref.dtype)


def _mm_res_kernel(x_ref, w_ref, r_ref, o_ref):
    o_ref[...] = r_ref[...] + jnp.dot(x_ref[...], w_ref[...],
                                      preferred_element_type=F32)


def _mm_swiglu_kernel(x_ref, wg_ref, wu_ref, o_ref):
    x = x_ref[...]
    g = jnp.dot(x, wg_ref[...], preferred_element_type=F32)
    u = jnp.dot(x, wu_ref[...], preferred_element_type=F32)
    o_ref[...] = (jax.nn.silu(g) * u).astype(o_ref.dtype)


def _matmul(x, w, *, tn, out_dtype, tm=1024, col_off=0, n_out=None, residual=None, name):
    m, k = x.shape
    n_out = w.shape[1] if n_out is None else n_out
    tm = min(m, tm)
    off = col_off // tn
    in_specs = [pl.BlockSpec((tm, k), lambda i, j: (i, 0)),
                pl.BlockSpec((k, tn), lambda i, j: (0, j + off))]
    args = [x, w]
    blocks = _nbytes((tm, k), BF16) + _nbytes((k, tn), BF16) + _nbytes((tm, tn), out_dtype)
    body = _mm_kernel
    if residual is not None:
        in_specs.append(pl.BlockSpec((tm, tn), lambda i, j: (i, j)))
        args.append(residual)
        blocks += _nbytes((tm, tn), F32)
        body = _mm_res_kernel
    return pl.pallas_call(
        body,
        grid=(m // tm, n_out // tn),
        in_specs=in_specs,
        out_specs=pl.BlockSpec((tm, tn), lambda i, j: (i, j)),
        out_shape=jax.ShapeDtypeStruct((m, n_out), out_dtype),
        compiler_params=_params(blocks, _nbytes((tm, tn), F32)),
        name=name,
    )(*args)


def _swiglu_up(x, w_up, *, tn, name):
    m, k = x.shape
    d_ff = w_up.shape[1] // 2
    tm = min(m, 1024)
    u_off = d_ff // tn
    blocks = _nbytes((tm, k), BF16) + 2 * _nbytes((k, tn), BF16) + _nbytes((tm, tn), BF16)
    return pl.pallas_call(
        _mm_swiglu_kernel,
        grid=(m // tm, d_ff // tn),
        in_specs=[pl.BlockSpec((tm, k), lambda i, j: (i, 0)),
                  pl.BlockSpec((k, tn), lambda i, j: (0, j)),
                  pl.BlockSpec((k, tn), lambda i, j: (0, j + u_off))],
        out_specs=pl.BlockSpec((tm, tn), lambda i, j: (i, j)),
        out_shape=jax.ShapeDtypeStruct((m, d_ff), BF16),
        compiler_params=_params(blocks, 3 * _nbytes((tm, tn), F32)),
        name=name,
    )(x, w_up, w_up)


def _moba_prompt_kernel(q_ref, k_ref, v_ref, o_ref, *, n_blk):
    t = n_blk * MOBA_BLOCK
    q = q_ref[...]
    k = k_ref[...]
    v = v_ref[...]

    k_mean = jnp.mean(k.reshape(n_blk, MOBA_BLOCK, HEAD_DIM), axis=1)
    gate = lax.dot_general(k_mean, q, NT_DIMS, precision=lax.Precision.HIGHEST,
                           preferred_element_type=F32)
    blk = lax.broadcasted_iota(jnp.int32, (n_blk, t), 0)
    own = lax.shift_right_logical(lax.broadcasted_iota(jnp.int32, (n_blk, t), 1),
                                  MOBA_BLOCK_LOG2)
    rank = jnp.zeros((n_blk, t), F32)
    for m in range(n_blk):
        gm = gate[m:m + 1, :]
        ahead = (gm > gate) | ((gm == gate) & (m < blk))
        rank = rank + jnp.where((own > m) & ahead, 1.0, 0.0)
    keep = ((blk < own) & (rank < MOBA_TOP_K)) | (blk == own)
    bias_t = jnp.where(keep, 0.0, MASKED)
    bias_t = jnp.concatenate([bias_t, jnp.zeros((HEAD_DIM - n_blk, t), F32)], axis=0)
    bias = bias_t.T

    q_aug = jnp.concatenate([(q * (HEAD_DIM ** -0.5)).astype(BF16), bias.astype(BF16)], axis=1)
    key_blk = lax.shift_right_logical(lax.broadcasted_iota(jnp.int32, (t, HEAD_DIM), 0),
                                      MOBA_BLOCK_LOG2)
    lane = lax.broadcasted_iota(jnp.int32, (t, HEAD_DIM), 1)
    k_aug = jnp.concatenate([k.astype(BF16), jnp.where(key_blk == lane, 1.0, 0.0).astype(BF16)],
                            axis=1)
    v_bf = v.astype(BF16)

    r_i = lax.broadcasted_iota(jnp.int32, (MOBA_BLOCK, MOBA_BLOCK), 0)
    c_i = lax.broadcasted_iota(jnp.int32, (MOBA_BLOCK, MOBA_BLOCK), 1)
    causal = c_i <= r_i
    for n in range(n_blk):
        lo = n * MOBA_BLOCK
        hi = lo + MOBA_BLOCK
        s = lax.dot_general(q_aug[lo:hi], k_aug[:hi], NT_DIMS, preferred_element_type=F32)
        s_own = jnp.where(causal, s[:, lo:], MASKED)
        s = s_own if n == 0 else jnp.concatenate([s[:, :lo], s_own], axis=1)
        p = jnp.exp(s - jnp.max(s, axis=1, keepdims=True))
        denom = jnp.sum(p, axis=1, keepdims=True)
        o = jnp.dot(p.astype(BF16), v_bf[:hi], preferred_element_type=F32)
        o_ref[lo:hi, :] = (o / denom).astype(o_ref.dtype)


def _moba_prompt(q, k, v):
    b, t, d = q.shape
    assert t % MOBA_BLOCK == 0 and t // MOBA_BLOCK <= SUBLANES
    n_blk = t // MOBA_BLOCK
    spec = pl.BlockSpec((None, t, HEAD_DIM), lambda i, h: (i, 0, h))
    blocks = 3 * _nbytes((t, HEAD_DIM), F32) + _nbytes((t, HEAD_DIM), BF16)
    return pl.pallas_call(
        functools.partial(_moba_prompt_kernel, n_blk=n_blk),
        grid=(b, d // HEAD_DIM),
        in_specs=[spec, spec, spec],
        out_specs=spec,
        out_shape=jax.ShapeDtypeStruct((b, t, d), BF16),
        compiler_params=_params(blocks, 16 << 20),
        name="moba_prompt",
    )(q, k, v)


PAGES_PER_STEP = 4


def _page_sum_kernel(pt_ref, *refs, pages_per_blk):
    page_refs, o_ref = refs[:-1], refs[-1]
    for blk in range(len(page_refs) // pages_per_blk):
        acc = None
        for p_ref in page_refs[blk * pages_per_blk:(blk + 1) * pages_per_blk]:
            part = jnp.sum(p_ref[...], axis=0, keepdims=True)
            acc = part if acc is None else acc + part
        o_ref[blk:blk + 1, :] = acc * (1.0 / MOBA_BLOCK)


def _past_block_means(cache, layer, page_table):
    _, _, page, d = cache.shape
    b, n_pages = page_table.shape
    pages_per_blk = MOBA_BLOCK // page
    blk_per_step = PAGES_PER_STEP // pages_per_blk
    n_steps = n_pages // PAGES_PER_STEP
    in_specs = [
        pl.BlockSpec((None, None, page, d),
                     lambda i, g, pt, p=p: (layer, pt[i * n_pages + g * PAGES_PER_STEP + p], 0, 0))
        for p in range(PAGES_PER_STEP)]
    out = pl.pallas_call(
        functools.partial(_page_sum_kernel, pages_per_blk=pages_per_blk),
        grid_spec=pltpu.PrefetchScalarGridSpec(
            num_scalar_prefetch=1,
            grid=(b, n_steps),
            in_specs=in_specs,
            out_specs=pl.BlockSpec((None, None, blk_per_step, d), lambda i, g, pt: (i, g, 0, 0)),
        ),
        out_shape=jax.ShapeDtypeStruct((b, n_steps, blk_per_step, d), F32),
        compiler_params=_params(PAGES_PER_STEP * _nbytes((page, d), F32)),
        name="past_block_means",
    )(page_table.reshape(-1), *([cache] * PAGES_PER_STEP))
    return out.reshape(b, n_steps * blk_per_step, d)


def _sample_topk_kernel(q_ref, km_ref, idx_ref):
    q = q_ref[...]
    km = km_ref[...]
    n_past = km.shape[0]
    g = lax.dot_general(q, km, NT_DIMS, precision=lax.Precision.HIGHEST,
                        preferred_element_type=F32)
    col = lax.broadcasted_iota(jnp.int32, g.shape, 1).astype(F32)
    out_lane = lax.broadcasted_iota(jnp.int32, idx_ref.shape, 1)
    out = jnp.zeros(idx_ref.shape, F32)
    for r in range(MOBA_TOP_K):
        best = jnp.max(g, axis=1, keepdims=True)
        idx = jnp.min(jnp.where(g == best, col, float(n_past)), axis=1, keepdims=True)
        out = jnp.where(out_lane == r, idx, out)
        g = jnp.where(col == idx, -jnp.inf, g)
    idx_ref[...] = out.astype(jnp.int32)


def _sample_topk(q, k_mean):
    b, tq, d = q.shape
    n_past = k_mean.shape[1]
    h = d // HEAD_DIM
    out = pl.pallas_call(
        _sample_topk_kernel,
        grid=(b, h),
        in_specs=[pl.BlockSpec((None, tq, HEAD_DIM), lambda i, j: (i, 0, j)),
                  pl.BlockSpec((None, n_past, HEAD_DIM), lambda i, j: (i, 0, j))],
        out_specs=pl.BlockSpec((None, None, tq, HEAD_DIM), lambda i, j: (i, j, 0, 0)),
        out_shape=jax.ShapeDtypeStruct((b, h, tq, HEAD_DIM), jnp.int32),
        name="sample_topk",
    )(q, k_mean)
    return out[..., :MOBA_TOP_K]


def _moba_sample_kernel(pt_ref, sel_ref, q_ref, kn_ref, vn_ref, *refs, tq, tiles_per_q):
    n_tiles = tq * tiles_per_q
    k_refs, v_refs, o_ref = refs[:n_tiles], refs[n_tiles:2 * n_tiles], refs[2 * n_tiles]
    scale = HEAD_DIM ** -0.5
    qb = q_ref[...].astype(BF16)
    kn = kn_ref[...].astype(BF16)
    vn = vn_ref[...].astype(BF16)
    r_i = lax.broadcasted_iota(jnp.int32, (tq, tq), 0)
    c_i = lax.broadcasted_iota(jnp.int32, (tq, tq), 1)
    s_new = lax.dot_general(qb, kn, NT_DIMS, preferred_element_type=F32) * scale
    s_new = jnp.where(c_i <= r_i, s_new, MASKED)
    m_new = jnp.max(s_new, axis=1, keepdims=True)
    row = lax.broadcasted_iota(jnp.int32, (tq, HEAD_DIM), 0)
    out = jnp.zeros((tq, HEAD_DIM), F32)
    for t in range(tq):
        tiles = slice(t * tiles_per_q, (t + 1) * tiles_per_q)
        kt = jnp.concatenate([r[...] for r in k_refs[tiles]], axis=0).astype(BF16)
        vt = jnp.concatenate([r[...] for r in v_refs[tiles]], axis=0).astype(BF16)
        s_past = lax.dot_general(qb, kt, NT_DIMS, preferred_element_type=F32) * scale
        m = jnp.maximum(jnp.max(s_past, axis=1, keepdims=True), m_new)
        p_past = jnp.exp(s_past - m)
        p_new = jnp.exp(s_new - m)
        denom = jnp.sum(p_past, axis=1, keepdims=True) + jnp.sum(p_new, axis=1, keepdims=True)
        o = (jnp.dot(p_past.astype(BF16), vt, preferred_element_type=F32)
             + jnp.dot(p_new.astype(BF16), vn, preferred_element_type=F32))
        out = jnp.where(row == t, o / denom, out)
    o_ref[...] = out


def _moba_sample(q, k_new, v_new, cache_k, cache_v, layer, page_table, sel):
    b, tq, d = q.shape
    h = d // HEAD_DIM
    page = cache_k.shape[2]
    n_pages = page_table.shape[1]
    pages_per_blk = MOBA_BLOCK // page
    tiles_per_q = MOBA_TOP_K * pages_per_blk

    def tile_spec(t, s, half):
        def index(i, j, pt, sl):
            blk = sl[((i * h + j) * tq + t) * MOBA_TOP_K + s]
            return (layer, pt[i * n_pages + blk * pages_per_blk + half], 0, j)
        return pl.BlockSpec((None, None, page, HEAD_DIM), index)

    tile_specs = [tile_spec(t, s, half) for t in range(tq) for s in range(MOBA_TOP_K)
                  for half in range(pages_per_blk)]
    tok_spec = pl.BlockSpec((None, tq, HEAD_DIM), lambda i, j, pt, sl: (i, 0, j))
    n_tiles = len(tile_specs)
    blocks = 2 * n_tiles * _nbytes((page, HEAD_DIM), F32) + 4 * _nbytes((tq, HEAD_DIM), F32)
    return pl.pallas_call(
        functools.partial(_moba_sample_kernel, tq=tq, tiles_per_q=tiles_per_q),
        grid_spec=pltpu.PrefetchScalarGridSpec(
            num_scalar_prefetch=2,
            grid=(b, h),
            in_specs=[tok_spec, tok_spec, tok_spec] + tile_specs + tile_specs,
            out_specs=tok_spec,
        ),
        out_shape=jax.ShapeDtypeStruct((b, tq, d), F32),
        compiler_params=_params(blocks, 8 << 20),
        name="moba_sample",
    )(page_table.reshape(-1), sel.reshape(-1), q, k_new, v_new,
      *([cache_k] * n_tiles), *([cache_v] * n_tiles))


def _rglru_kernel(gate_ref, u_ref, cprev_ref, hprev_ref, cw_ref, cb_ref, wa_ref, ba_ref,
                  wi_ref, bi_ref, lam_ref, y_ref, conv_ref, hlast_ref, uext_sc, a_sc, h_sc):
    t = u_ref.shape[0]
    u = u_ref[...]
    uext_sc[0:SUBLANES, :] = cprev_ref[...]
    uext_sc[SUBLANES:SUBLANES + t, :] = u
    cw = cw_ref[...]
    uc = cb_ref[...]
    for j in range(CONV_WIDTH - 1):
        start = SUBLANES - (CONV_WIDTH - 1) + j
        uc = uc + uext_sc[start:start + t, :] * cw[j:j + 1, :]
    uc = uc + u * cw[CONV_WIDTH - 1:CONV_WIDTH, :]
    conv_ref[...] = uext_sc[t:t + SUBLANES, :]

    uc_bf = uc.astype(BF16)
    r = jax.nn.sigmoid(jnp.dot(uc_bf, wa_ref[...], preferred_element_type=F32) + ba_ref[...])
    i = jax.nn.sigmoid(jnp.dot(uc_bf, wi_ref[...], preferred_element_type=F32) + bi_ref[...])
    lam = lam_ref[...]
    softplus_neg_lam = jnp.maximum(-lam, 0.0) + jnp.log1p(jnp.exp(-jnp.abs(lam)))
    log_a = (-LRU_C * r) * softplus_neg_lam
    a_sc[...] = jnp.exp(log_a)
    th = jnp.tanh(log_a)
    h_sc[...] = jnp.sqrt(-2.0 * th / (1.0 - th)) * (i * uc)

    row = lax.broadcasted_iota(jnp.int32, (SUBLANES, LRU_BLOCK), 0)

    def group(c, carry):
        off = pl.multiple_of(c * SUBLANES, SUBLANES)
        a = a_sc[pl.ds(off, SUBLANES), :]
        b = h_sc[pl.ds(off, SUBLANES), :]
        for d in (1, 2, 4):
            a_prev = jnp.where(row >= d, pltpu.roll(a, d, 0), 1.0)
            b_prev = jnp.where(row >= d, pltpu.roll(b, d, 0), 0.0)
            b = a * b_prev + b
            a = a * a_prev
        h = a * carry + b
        h_sc[pl.ds(off, SUBLANES), :] = h
        return h[SUBLANES - 1:SUBLANES, :]

    lax.fori_loop(0, t // SUBLANES, group, hprev_ref[...])
    h = h_sc[...]
    y_ref[...] = (h * jax.nn.gelu(gate_ref[...])).astype(y_ref.dtype)
    hlast_ref[...] = h_sc[t - SUBLANES:t, :]


def _rglru(gu, conv_prev, h_prev, conv_w, conv_b, w_a, b_a, w_i, b_i, lam):
    b, t, w2 = gu.shape
    w = w2 // 2
    n_blk = w // LRU_BLOCK
    assert t % SUBLANES == 0 and t >= SUBLANES
    keep = CONV_WIDTH - 1
    cprev8 = jnp.concatenate([jnp.zeros((b, SUBLANES - keep, w), F32), conv_prev], axis=1)
    vec = lambda a: a.reshape(1, w)
    vec_spec = pl.BlockSpec((1, LRU_BLOCK), lambda i, n: (0, n))
    tile = lambda rows: pl.BlockSpec((None, rows, LRU_BLOCK), lambda i, n: (i, 0, n))
    gate_w_spec = pl.BlockSpec((None, LRU_BLOCK, LRU_BLOCK), lambda i, n: (n, 0, 0))
    blocks = (2 * _nbytes((t, LRU_BLOCK), F32) + _nbytes((t, LRU_BLOCK), BF16)
              + 2 * _nbytes((LRU_BLOCK, LRU_BLOCK), BF16))
    y, conv8, h8 = pl.pallas_call(
        _rglru_kernel,
        grid=(b, n_blk),
        in_specs=[tile(t),
                  pl.BlockSpec((None, t, LRU_BLOCK), lambda i, n: (i, 0, n + n_blk)),
                  tile(SUBLANES), tile(1),
                  pl.BlockSpec((CONV_WIDTH, LRU_BLOCK), lambda i, n: (0, n)), vec_spec,
                  gate_w_spec, vec_spec, gate_w_spec, vec_spec, vec_spec],
        out_specs=[tile(t), tile(SUBLANES), tile(SUBLANES)],
        out_shape=[jax.ShapeDtypeStruct((b, t, w), BF16),
                   jax.ShapeDtypeStruct((b, SUBLANES, w), F32),
                   jax.ShapeDtypeStruct((b, SUBLANES, w), F32)],
        scratch_shapes=[pltpu.VMEM((t + SUBLANES, LRU_BLOCK), F32),
                        pltpu.VMEM((t, LRU_BLOCK), F32),
                        pltpu.VMEM((t, LRU_BLOCK), F32)],
        compiler_params=_params(blocks, 12 * _nbytes((t, LRU_BLOCK), F32)),
        name="rglru",
    )(gu, gu, cprev8, h_prev.reshape(b, 1, w), conv_w, vec(conv_b),
      w_a.astype(BF16), vec(b_a), w_i.astype(BF16), vec(b_i), vec(lam))
    return y, conv8[:, SUBLANES - keep:], h8[:, SUBLANES - 1]


def kernel(x_prompt, x_sample, cache_k, cache_v, page_table, state_conv, state_h, norm_mix, norm_ffn, norm_final, w_qkv, w_o, w_rec_in, rec_conv_w, rec_conv_b, w_rec_a, b_rec_a, w_rec_i, b_rec_i, rec_lambda, w_rec_out, w_ffn_up, w_ffn_down):
    bp, tp, d = x_prompt.shape
    bs, ts, _ = x_sample.shape
    depth = norm_mix.shape[0]
    n_layers, n_pool, page, n_heads, head_dim = cache_k.shape
    assert (n_heads, head_dim) == (N_HEADS, HEAD_DIM) and MOBA_BLOCK % page == 0
    assert (page_table.shape[1] * page) % MOBA_BLOCK == 0 and ts <= MOBA_BLOCK
    assert page_table.shape[1] % PAGES_PER_STEP == 0
    assert page_table.shape[1] * page // MOBA_BLOCK >= MOBA_TOP_K
    w_lru = state_h.shape[-1]
    ck = cache_k.reshape(n_layers, n_pool, page, d)
    cv = cache_v.reshape(n_layers, n_pool, page, d)

    xp = x_prompt.reshape(bp * tp, d)
    xs = x_sample.reshape(bs * ts, d)
    kp_l, vp_l, ks_l, vs_l, cp_l, hp_l, cs_l, hs_l = ([] for _ in range(8))
    for layer in range(depth):
        j = layer // 2
        np_ = _rmsnorm(xp, norm_mix[layer], BF16)
        ns_ = _rmsnorm(xs, norm_mix[layer], BF16)
        if layer % 2 == 0:
            wqkv = w_qkv[j].astype(BF16)
            wo = w_o[j].astype(BF16)
            qkv_p = [_matmul(np_, wqkv, tn=1024, out_dtype=F32, col_off=c * d, n_out=d,
                             name="qkv_prompt") for c in range(3)]
            qkv_s = [_matmul(ns_, wqkv, tn=1024, out_dtype=F32, col_off=c * d, n_out=d,
                             name="qkv_sample") for c in range(3)]
            qp, kp, vp = (a.reshape(bp, tp, d) for a in qkv_p)
            qs, k_s, v_s = (a.reshape(bs, ts, d) for a in qkv_s)
            op = _moba_prompt(qp, kp, vp).reshape(bp * tp, d)
            k_mean = _past_block_means(ck, j, page_table)
            sel = _sample_topk(qs, k_mean)
            o_s = _moba_sample(qs, k_s, v_s, ck, cv, j, page_table, sel)
            o_s = o_s.reshape(bs * ts, d).astype(BF16)
            xp = _matmul(op, wo, tn=512, out_dtype=F32, residual=xp, name="wo_prompt")
            xs = _matmul(o_s, wo, tn=512, out_dtype=F32, residual=xs, name="wo_sample")
            kp_l.append(kp.reshape(bp, tp, N_HEADS, HEAD_DIM))
            vp_l.append(vp.reshape(bp, tp, N_HEADS, HEAD_DIM))
            ks_l.append(k_s.reshape(bs, ts, N_HEADS, HEAD_DIM))
            vs_l.append(v_s.reshape(bs, ts, N_HEADS, HEAD_DIM))
        else:
            w_in = w_rec_in[j].astype(BF16)
            w_out = w_rec_out[j].astype(BF16)
            prm = (rec_conv_w[j], rec_conv_b[j], w_rec_a[j], b_rec_a[j], w_rec_i[j], b_rec_i[j],
                   rec_lambda[j])
            gu_p = _matmul(np_, w_in, tn=1024, out_dtype=F32, name="rec_in_prompt")
            gu_s = _matmul(ns_, w_in, tn=1024, out_dtype=F32, name="rec_in_sample")
            conv0 = jnp.zeros((bp, CONV_WIDTH - 1, w_lru), F32)
            h0 = jnp.zeros((bp, w_lru), F32)
            yp, cp, hp = _rglru(gu_p.reshape(bp, tp, 2 * w_lru), conv0, h0, *prm)
            ys, c_s, h_s = _rglru(gu_s.reshape(bs, ts, 2 * w_lru), state_conv[j], state_h[j], *prm)
            xp = _matmul(yp.reshape(bp * tp, w_lru), w_out, tn=512, out_dtype=F32, residual=xp,
                         name="rec_out_prompt")
            xs = _matmul(ys.reshape(bs * ts, w_lru), w_out, tn=512, out_dtype=F32, residual=xs,
                         name="rec_out_sample")
            cp_l.append(cp); hp_l.append(hp); cs_l.append(c_s); hs_l.append(h_s)
        w_up = w_ffn_up[layer].astype(BF16)
        w_down = w_ffn_down[layer].astype(BF16)
        hp_ = _swiglu_up(_rmsnorm(xp, norm_ffn[layer], BF16), w_up, tn=256, name="ffn_up_prompt")
        hs_ = _swiglu_up(_rmsnorm(xs, norm_ffn[layer], BF16), w_up, tn=256, name="ffn_up_sample")
        xp = _matmul(hp_, w_down, tm=512, tn=512, out_dtype=F32, residual=xp,
                     name="ffn_down_prompt")
        xs = _matmul(hs_, w_down, tn=512, out_dtype=F32, residual=xs, name="ffn_down_sample")
    y_prompt = _rmsnorm(xp, norm_final, F32).reshape(bp, tp, d)
    y_sample = _rmsnorm(xs, norm_final, F32).reshape(bs, ts, d)
    stack = lambda xs_: xs_[0][None] if len(xs_) == 1 else jnp.stack(xs_)
    return (y_prompt, y_sample) + tuple(
        stack(l) for l in (kp_l, vp_l, ks_l, vs_l, cp_l, hp_l, cs_l, hs_l))
```
